```python
import jax, jax.numpy as jnp
from jax import lax
import numpy as np

D_MODEL = 1024
BATCH = 32
SEQ = 2048
DEPTH = 4
DEC_BATCH = 16
DEC_SEQ = 16
PAST_LEN = 2048

CHUNK = 64
Q_BLOCK = 128
N_HEADS = 8
QK_NOPE_DIM = 64
QK_ROPE_DIM = 32
V_HEAD_DIM = 64
Q_LORA = 384
KV_LORA = 256
ROPE_THETA = 10000.0
ATTN_SCALE = (QK_NOPE_DIM + QK_ROPE_DIM) ** -0.5
CONF_CH = 256
CONF_WIDTH = 31
SC_CH = 256
SC_WIDTH = 3
N_BRANCH = 3
D_FF = 4 * D_MODEL
DN_ALPHA = (2 * DEPTH) ** 0.25
DN_BETA = (8 * DEPTH) ** -0.25
LN_EPS = 1e-5
RMS_EPS = 1e-6
SPLITS = (Q_LORA, KV_LORA, QK_ROPE_DIM, 2 * CONF_CH, 3 * SC_CH, N_BRANCH * D_MODEL)
N_IN = Q_LORA + KV_LORA + QK_ROPE_DIM + 2 * CONF_CH + 3 * SC_CH + N_BRANCH * D_MODEL

kernel_name = "mla_conformer_shortconv_deepnorm_step"


def layer_norm(x, g, b):
    xf = x.astype(jnp.float32)
    mu = xf.mean(-1, keepdims=True)
    var = jnp.square(xf - mu).mean(-1, keepdims=True)
    return ((xf - mu) * lax.rsqrt(var + LN_EPS) * g.astype(jnp.float32) + b.astype(jnp.float32)).astype(x.dtype)


def rms_norm(x, g):
    xf = x.astype(jnp.float32)
    ms = jnp.square(xf).mean(-1, keepdims=True)
    return (xf * lax.rsqrt(ms + RMS_EPS) * g.astype(jnp.float32)).astype(x.dtype)


def rope(x, pos):
    half = QK_ROPE_DIM // 2
    inv = ROPE_THETA ** (-jnp.arange(half, dtype=jnp.float32) / half)
    ang = pos.astype(jnp.float32)[:, None] * inv[None, :]
    cos = jnp.cos(ang)[None, :, None, :]
    sin = jnp.sin(ang)[None, :, None, :]
    xf = x.astype(jnp.float32)
    x1, x2 = xf[..., :half], xf[..., half:]
    return jnp.concatenate([x1 * cos - x2 * sin, x1 * sin + x2 * cos], axis=-1).astype(x.dtype)


def depthwise_conv(xp, w):
    c = xp.shape[-1]
    return lax.conv_general_dilated(xp, w[:, None, :].astype(xp.dtype), window_strides=(1,), padding='VALID',
                                    dimension_numbers=('NWC', 'WIO', 'NWC'), feature_group_count=c)


def mla_block(q_nope, q_rope, q_pos, ckv, krope, k_pos, w_uk, w_uv):
    q_lat = jnp.einsum('bqhd,chd->bqhc', q_nope, w_uk)
    s = jnp.einsum('bqhc,bkc->bhqk', q_lat, ckv) + jnp.einsum('bqhr,bkr->bhqk', q_rope, krope)
    s = s.astype(jnp.float32) * ATTN_SCALE
    mask = (k_pos[None, :] // CHUNK) <= (q_pos[:, None] // CHUNK)
    s = jnp.where(mask[None, None], s, -1e30)
    p = jax.nn.softmax(s, axis=-1).astype(ckv.dtype)
    o_lat = jnp.einsum('bhqk,bkc->bqhc', p, ckv)
    return jnp.einsum('bqhc,chd->bqhd', o_lat, w_uv)


def mla_attention(q_nope, q_rope, q_pos, ckv, krope, k_pos, w_uk, w_uv):
    b, l = q_nope.shape[:2]
    if l > Q_BLOCK and l % Q_BLOCK == 0:
        nb = l // Q_BLOCK

        def blk(a):
            return a.reshape((b, nb, Q_BLOCK) + a.shape[2:]).swapaxes(0, 1)

        out = lax.map(lambda t: mla_block(t[0], t[1], t[2], ckv, krope, k_pos, w_uk, w_uv),
                      (blk(q_nope), blk(q_rope), q_pos.reshape(nb, Q_BLOCK)))
        return out.swapaxes(0, 1).reshape(b, l, N_HEADS, V_HEAD_DIM)
    return mla_block(q_nope, q_rope, q_pos, ckv, krope, k_pos, w_uk, w_uv)


def trunk_layer(x, past_ckv, past_krope, past_conf, past_sc, p):
    (w_in, b_gate, q_norm_g, w_uq, kv_norm_g, w_uk, w_uv, w_mla_out, conf_dw_w, conf_dw_b, conf_ln_g, conf_ln_b,
     w_conf_out, sc_dw_w, w_sc_out, w_mix_out, ln1_g, ln1_b, w_ff1, b_ff1, w_ff2, b_ff2, ln2_g, ln2_b) = p
    b, l, _ = x.shape
    past = past_ckv.shape[1]
    proj = x @ w_in
    offs = [int(o) for o in np.cumsum(SPLITS)[:-1]]
    q_c, kv_c, k_r, conf_in, sc_in, gate_in = jnp.split(proj, offs, axis=-1)
    pos = past + jnp.arange(l)

    q = jnp.einsum('blc,chd->blhd', rms_norm(q_c, q_norm_g), w_uq)
    q_nope = q[..., :QK_NOPE_DIM]
    q_rope = rope(q[..., QK_NOPE_DIM:], pos)
    c_kv = rms_norm(kv_c, kv_norm_g)
    k_rope = rope(k_r[:, :, None, :], pos)[:, :, 0]
    keys_ckv = jnp.concatenate([past_ckv, c_kv], axis=1)
    keys_kr = jnp.concatenate([past_krope, k_rope], axis=1)
    k_pos = jnp.arange(past + l)
    o = mla_attention(q_nope, q_rope, pos, keys_ckv, keys_kr, k_pos, w_uk, w_uv)
    mla_out = o.reshape(b, l, N_HEADS * V_HEAD_DIM) @ w_mla_out

    ca, cg = jnp.split(conf_in, 2, axis=-1)
    u = ca * jax.nn.sigmoid(cg)
    up = jnp.concatenate([past_conf, u], axis=1)
    new_conf = up[:, -(CONF_WIDTH - 1):]
    cv = depthwise_conv(up, conf_dw_w) + conf_dw_b
    conf_out = jax.nn.silu(layer_norm(cv, conf_ln_g, conf_ln_b)) @ w_conf_out

    gb, gc, h = jnp.split(sc_in, 3, axis=-1)
    z = gc * h
    zp = jnp.concatenate([past_sc, z], axis=1)
    new_sc = zp[:, -(SC_WIDTH - 1):]
    sc_out = (gb * depthwise_conv(zp, sc_dw_w)) @ w_sc_out

    g = jax.nn.sigmoid(gate_in.reshape(b, l, N_BRANCH, D_MODEL) + b_gate)
    merged = g[:, :, 0] * mla_out + g[:, :, 1] * conf_out + g[:, :, 2] * sc_out
    x = layer_norm(DN_ALPHA * x + merged @ w_mix_out, ln1_g, ln1_b)

    hdn = jnp.square(jax.nn.relu(x @ w_ff1 + b_ff1))
    x = layer_norm(DN_ALPHA * x + (hdn @ w_ff2 + b_ff2), ln2_g, ln2_b)
    return x, c_kv, k_rope, new_conf, new_sc


def setup_inputs(seed: int = 0) -> dict:
    key = jax.random.key(seed)
    ks = jax.random.split(key, 32)
    f32 = jnp.float32

    def nrm(k, shape, scale):
        return jax.random.normal(k, shape, f32) * scale

    return {
        "x_prompt": nrm(ks[0], (BATCH, SEQ, D_MODEL), 1.0),
        "x_sample": nrm(ks[1], (DEC_BATCH, DEC_SEQ, D_MODEL), 1.0),
        "cache_ckv": nrm(ks[2], (DEPTH, DEC_BATCH, PAST_LEN, KV_LORA), 1.0),
        "cache_krope": nrm(ks[3], (DEPTH, DEC_BATCH, PAST_LEN, QK_ROPE_DIM), 1.0),
        "state_conf": nrm(ks[4], (DEPTH, DEC_BATCH, CONF_WIDTH - 1, CONF_CH), 0.5),
        "state_sc": nrm(ks[5], (DEPTH, DEC_BATCH, SC_WIDTH - 1, SC_CH), 0.5),
        "w_in": nrm(ks[6], (DEPTH, D_MODEL, N_IN), D_MODEL ** -0.5),
        "b_gate": nrm(ks[7], (DEPTH, N_BRANCH, D_MODEL), 0.02),
        "q_norm_g": 1.0 + nrm(ks[8], (DEPTH, Q_LORA), 0.02),
        "w_uq": nrm(ks[9], (DEPTH, Q_LORA, N_HEADS, QK_NOPE_DIM + QK_ROPE_DIM), Q_LORA ** -0.5),
        "kv_norm_g": 1.0 + nrm(ks[10], (DEPTH, KV_LORA), 0.02),
        "w_uk": nrm(ks[11], (DEPTH, KV_LORA, N_HEADS, QK_NOPE_DIM), KV_LORA ** -0.5),
        "w_uv": nrm(ks[12], (DEPTH, KV_LORA, N_HEADS, V_HEAD_DIM), DN_BETA * KV_LORA ** -0.5),
        "w_mla_out": nrm(ks[13], (DEPTH, N_HEADS * V_HEAD_DIM, D_MODEL), (N_HEADS * V_HEAD_DIM) ** -0.5),
        "conf_dw_w": nrm(ks[14], (DEPTH, CONF_WIDTH, CONF_CH), CONF_WIDTH ** -0.5),
        "conf_dw_b": nrm(ks[15], (DEPTH, CONF_CH), 0.02),
        "conf_ln_g": 1.0 + nrm(ks[16], (DEPTH, CONF_CH), 0.02),
        "conf_ln_b": nrm(ks[17], (DEPTH, CONF_CH), 0.02),
        "w_conf_out": nrm(ks[18], (DEPTH, CONF_CH, D_MODEL), CONF_CH ** -0.5),
        "sc_dw_w": nrm(ks[19], (DEPTH, SC_WIDTH, SC_CH), SC_WIDTH ** -0.5),
        "w_sc_out": nrm(ks[20], (DEPTH, SC_CH, D_MODEL), SC_CH ** -0.5),
        "w_mix_out": nrm(ks[21], (DEPTH, D_MODEL, D_MODEL), DN_BETA * D_MODEL ** -0.5),
        "ln1_g": 1.0 + nrm(ks[22], (DEPTH, D_MODEL), 0.02),
        "ln1_b": nrm(ks[23], (DEPTH, D_MODEL), 0.02),
        "w_ff1": nrm(ks[24], (DEPTH, D_MODEL, D_FF), DN_BETA * D_MODEL ** -0.5),
        "b_ff1": nrm(ks[25], (DEPTH, D_FF), 0.02),
        "w_ff2": nrm(ks[26], (DEPTH, D_FF, D_MODEL), DN_BETA * D_FF ** -0.5),
        "b_ff2": nrm(ks[27], (DEPTH, D_MODEL), 0.02),
        "ln2_g": 1.0 + nrm(ks[28], (DEPTH, D_MODEL), 0.02),
        "ln2_b": nrm(ks[29], (DEPTH, D_MODEL), 0.02),
    }


def reference(x_prompt, x_sample, cache_ckv, cache_krope, state_conf, state_sc, w_in, b_gate, q_norm_g, w_uq,
              kv_norm_g, w_uk, w_uv, w_mla_out, conf_dw_w, conf_dw_b, conf_ln_g, conf_ln_b, w_conf_out, sc_dw_w,
              w_sc_out, w_mix_out, ln1_g, ln1_b, w_ff1, b_ff1, w_ff2, b_ff2, ln2_g, ln2_b):
    def layer_params(i):
        return (w_in[i], b_gate[i], q_norm_g[i], w_uq[i], kv_norm_g[i], w_uk[i], w_uv[i], w_mla_out[i],
                conf_dw_w[i], conf_dw_b[i], conf_ln_g[i], conf_ln_b[i], w_conf_out[i], sc_dw_w[i], w_sc_out[i],
                w_mix_out[i], ln1_g[i], ln1_b[i], w_ff1[i], b_ff1[i], w_ff2[i], b_ff2[i], ln2_g[i], ln2_b[i])

    def run(x, past_ckv, past_krope, past_conf, past_sc):
        ckvs, krs, confs, scs = [], [], [], []
        for i in range(DEPTH):
            x, c_kv, k_rope, n_conf, n_sc = trunk_layer(x, past_ckv[i], past_krope[i], past_conf[i], past_sc[i],
                                                         layer_params(i))
            ckvs.append(c_kv)
            krs.append(k_rope)
            confs.append(n_conf)
            scs.append(n_sc)
        return x, jnp.stack(ckvs), jnp.stack(krs), jnp.stack(confs), jnp.stack(scs)

    bp = x_prompt.shape[0]
    dt = x_prompt.dtype
    y_prompt, ckv_p, kr_p, conf_p, sc_p = run(
        x_prompt,
        jnp.zeros((DEPTH, bp, 0, KV_LORA), dt),
        jnp.zeros((DEPTH, bp, 0, QK_ROPE_DIM), dt),
        jnp.zeros((DEPTH, bp, CONF_WIDTH - 1, CONF_CH), dt),
        jnp.zeros((DEPTH, bp, SC_WIDTH - 1, SC_CH), dt))
    y_sample, ckv_s, kr_s, conf_s, sc_s = run(x_sample, cache_ckv, cache_krope, state_conf, state_sc)
    return (y_prompt, y_sample, ckv_p, kr_p, conf_p, sc_p, ckv_s, kr_s, conf_s, sc_s)
```

```python
import functools

import jax
import jax.numpy as jnp
from jax import lax
from jax.experimental import pallas as pl
from jax.experimental.pallas import tpu as pltpu

CHUNK = 64
CHUNK_SHIFT = 6
assert 1 << CHUNK_SHIFT == CHUNK
N_HEADS = 8
QK_NOPE_DIM = 64
QK_ROPE_DIM = 32
V_HEAD_DIM = 64
Q_LORA = 384
KV_LORA = 256
ROPE_THETA = 10000.0
ATTN_SCALE = (QK_NOPE_DIM + QK_ROPE_DIM) ** -0.5
CONF_CH = 256
CONF_WIDTH = 31
SC_CH = 256
SC_WIDTH = 3
N_BRANCH = 3
LN_EPS = 1e-5
RMS_EPS = 1e-6
MASK_VALUE = -1e30

LANES = 128
HEAD_PAD = LANES
ROPE_OFF = QK_NOPE_DIM
CONF_HALO = 32
SC_HALO = 8
VMEM_LIMIT = 56 * 1024 * 1024

BF16 = jnp.bfloat16
F32 = jnp.float32


def _dot(a, b):
    return jnp.dot(a, b, preferred_element_type=F32)


def _dot_nt(a, b):
    return lax.dot_general(a, b, (((1,), (1,)), ((), ())), preferred_element_type=F32)


def _sigmoid(x):
    return 1.0 / (1.0 + jnp.exp(-x))


def _layer_norm(x, g, b):
    mu = jnp.mean(x, axis=-1, keepdims=True)
    xc = x - mu
    var = jnp.mean(xc * xc, axis=-1, keepdims=True)
    return xc * lax.rsqrt(var + LN_EPS) * g + b


def _rms_norm(x, g):
    ms = jnp.mean(x * x, axis=-1, keepdims=True)
    return x * lax.rsqrt(ms + RMS_EPS) * g


def _rope_slab(x, cc, sa, sb):
    up = pltpu.roll(x, LANES - QK_ROPE_DIM // 2, 1)
    down = pltpu.roll(x, QK_ROPE_DIM // 2, 1)
    return x * cc + up * sa + down * sb


def _mixer_in_kernel(x_ref, cc_ref, sa_ref, sb_ref, conf0_ref, sc0_ref, wa_ref, qg_ref, wq_ref, kvg_ref, wkv_ref,
                     cw_ref, cb_ref, clg_ref, clb_ref, sw_ref,
                     q_ref, k_ref, v_ref, ckv_ref, kr_ref, br_ref, nconf_ref, nsc_ref,
                     ubuf, zbuf, *, bt, tl):
    lt = pl.program_id(1)
    rows = bt * tl
    d_model = x_ref.shape[-1]

    @pl.when(lt == 0)
    def _():
        ubuf[:, CONF_HALO - (CONF_WIDTH - 1):CONF_HALO, :] = conf0_ref[...]
        zbuf[:, SC_HALO - (SC_WIDTH - 1):SC_HALO, :] = sc0_ref[...]

    x = x_ref[...].reshape(rows, d_model)
    proj = _dot(x.astype(BF16), wa_ref[...])
    o_kv = Q_LORA
    o_kr = o_kv + KV_LORA
    o_conf = o_kr + LANES
    o_sc = o_conf + 2 * CONF_CH
    cc = jnp.concatenate([cc_ref[...]] * bt, axis=0) if bt > 1 else cc_ref[...]
    sa = jnp.concatenate([sa_ref[...]] * bt, axis=0) if bt > 1 else sa_ref[...]
    sb = jnp.concatenate([sb_ref[...]] * bt, axis=0) if bt > 1 else sb_ref[...]

    qn = _rms_norm(proj[:, :Q_LORA], qg_ref[...])
    q_pre = _dot(qn.astype(BF16), wq_ref[...])
    for h in range(N_HEADS):
        sl = slice(h * HEAD_PAD, (h + 1) * HEAD_PAD)
        q_ref[:, :, sl] = _rope_slab(q_pre[:, sl], cc, sa, sb).astype(BF16).reshape(bt, tl, HEAD_PAD)
    c_kv = _rms_norm(proj[:, o_kv:o_kr], kvg_ref[...])
    ckv_ref[...] = c_kv.reshape(bt, tl, KV_LORA)
    kr = _rope_slab(proj[:, o_kr:o_conf], cc, sa, sb)
    kr_ref[...] = kr[:, ROPE_OFF:ROPE_OFF + QK_ROPE_DIM].reshape(bt, tl, QK_ROPE_DIM)
    kv_up = _dot(c_kv.astype(BF16), wkv_ref[...])
    for h in range(N_HEADS):
        sl = slice(h * HEAD_PAD, (h + 1) * HEAD_PAD)
        k_ref[:, :, sl] = (kv_up[:, sl] + kr).astype(BF16).reshape(bt, tl, HEAD_PAD)
    v_ref[...] = kv_up[:, N_HEADS * HEAD_PAD:].astype(BF16).reshape(bt, tl, N_HEADS * V_HEAD_DIM)

    u = proj[:, o_conf:o_conf + CONF_CH] * _sigmoid(proj[:, o_conf + CONF_CH:o_sc])
    ubuf[:, CONF_HALO:CONF_HALO + tl, :] = u.reshape(bt, tl, CONF_CH)
    base = CONF_HALO - (CONF_WIDTH - 1)
    cv = jnp.zeros((bt, tl, CONF_CH), F32)
    for kk in range(CONF_WIDTH):
        cv = cv + ubuf[:, base + kk:base + kk + tl, :] * cw_ref[kk:kk + 1, :]
    cv = cv.reshape(rows, CONF_CH) + cb_ref[...]
    cn = _layer_norm(cv, clg_ref[...], clb_ref[...])
    br_ref[:, :, :CONF_CH] = (cn * _sigmoid(cn)).astype(BF16).reshape(bt, tl, CONF_CH)
    tail = ubuf[:, tl:tl + CONF_HALO, :]
    ubuf[:, :CONF_HALO, :] = tail
    nconf_ref[...] = tail[:, CONF_HALO - (CONF_WIDTH - 1):, :]

    gb = proj[:, o_sc:o_sc + SC_CH]
    z = proj[:, o_sc + SC_CH:o_sc + 2 * SC_CH] * proj[:, o_sc + 2 * SC_CH:o_sc + 3 * SC_CH]
    zbuf[:, SC_HALO:SC_HALO + tl, :] = z.reshape(bt, tl, SC_CH)
    base = SC_HALO - (SC_WIDTH - 1)
    sv = jnp.zeros((bt, tl, SC_CH), F32)
    for kk in range(SC_WIDTH):
        sv = sv + zbuf[:, base + kk:base + kk + tl, :] * sw_ref[kk:kk + 1, :]
    br_ref[:, :, CONF_CH:] = (gb * sv.reshape(rows, SC_CH)).astype(BF16).reshape(bt, tl, SC_CH)
    tail = zbuf[:, tl:tl + SC_HALO, :]
    zbuf[:, :SC_HALO, :] = tail
    nsc_ref[...] = tail[:, SC_HALO - (SC_WIDTH - 1):, :]


def _mixer_in(x, rope_tabs, conf0, sc0, wl, *, bt, tl):
    b, l, d = x.shape
    grid = (b // bt, l // tl)

    def full(a):
        return pl.BlockSpec(a.shape, lambda i, j: (0,) * a.ndim)

    def seq(c):
        return pl.BlockSpec((bt, tl, c), lambda i, j: (i, j, 0))

    def state(r, c):
        return pl.BlockSpec((bt, r, c), lambda i, j: (i, 0, 0))

    tab = pl.BlockSpec((tl, LANES), lambda i, j: (j, 0))
    weights = (wl["w_a"], wl["q_g"], wl["w_q"], wl["kv_g"], wl["w_kv"], wl["conf_w"], wl["conf_b"], wl["conf_lg"],
               wl["conf_lb"], wl["sc_w"])
    out_shape = (
        jax.ShapeDtypeStruct((b, l, N_HEADS * HEAD_PAD), BF16),
        jax.ShapeDtypeStruct((b, l, N_HEADS * HEAD_PAD), BF16),
        jax.ShapeDtypeStruct((b, l, N_HEADS * V_HEAD_DIM), BF16),
        jax.ShapeDtypeStruct((b, l, KV_LORA), F32),
        jax.ShapeDtypeStruct((b, l, QK_ROPE_DIM), F32),
        jax.ShapeDtypeStruct((b, l, CONF_CH + SC_CH), BF16),
        jax.ShapeDtypeStruct((b, CONF_WIDTH - 1, CONF_CH), F32),
        jax.ShapeDtypeStruct((b, SC_WIDTH - 1, SC_CH), F32),
    )
    out_specs = (seq(N_HEADS * HEAD_PAD), seq(N_HEADS * HEAD_PAD), seq(N_HEADS * V_HEAD_DIM), seq(KV_LORA),
                 seq(QK_ROPE_DIM), seq(CONF_CH + SC_CH), state(CONF_WIDTH - 1, CONF_CH), state(SC_WIDTH - 1, SC_CH))
    return pl.pallas_call(
        functools.partial(_mixer_in_kernel, bt=bt, tl=tl),
        grid=grid,
        in_specs=[seq(d), tab, tab, tab, state(CONF_WIDTH - 1, CONF_CH), state(SC_WIDTH - 1, SC_CH)]
        + [full(w) for w in weights],
        out_specs=out_specs,
        out_shape=out_shape,
        scratch_shapes=[pltpu.VMEM((bt, CONF_HALO + tl, CONF_CH), F32), pltpu.VMEM((bt, SC_HALO + tl, SC_CH), F32)],
        compiler_params=pltpu.CompilerParams(dimension_semantics=("arbitrary", "arbitrary"),
                                             vmem_limit_bytes=VMEM_LIMIT),
        name="mixer_in",
    )(x, *rope_tabs, conf0, sc0, *weights)


def _chunk_of(pos):
    return pos >> CHUNK_SHIFT


def _chunk_mask(tq, tk):
    qc = _chunk_of(lax.broadcasted_iota(jnp.int32, (tq, tk), 0))
    kc = _chunk_of(lax.broadcasted_iota(jnp.int32, (tq, tk), 1))
    return kc <= qc


def _flash_kernel(q_ref, k_ref, v_ref, o_ref, m_sc, l_sc, acc_sc, *, t):
    qi = pl.program_id(1)
    for h in range(N_HEADS):
        hs = slice(h * HEAD_PAD, (h + 1) * HEAD_PAD)
        vs = slice(h * V_HEAD_DIM, (h + 1) * V_HEAD_DIM)
        q_h = q_ref[0, :, hs]
        m_sc[...] = jnp.full(m_sc.shape, MASK_VALUE, F32)
        l_sc[...] = jnp.zeros(l_sc.shape, F32)
        acc_sc[...] = jnp.zeros(acc_sc.shape, F32)

        def step(j, masked):
            ks = pl.ds(pl.multiple_of(j * t, t), t)
            s = _dot_nt(q_h, k_ref[0, ks, hs]) * ATTN_SCALE
            if masked:
                s = jnp.where(_chunk_mask(t, t), s, MASK_VALUE)
            m_old = m_sc[...]
            m_new = jnp.maximum(m_old, jnp.max(s, axis=-1, keepdims=True))
            alpha = jnp.exp(m_old - m_new)
            p = jnp.exp(s - m_new)
            l_sc[...] = alpha * l_sc[...] + jnp.sum(p, axis=-1, keepdims=True)
            acc_sc[...] = alpha * acc_sc[...] + _dot(p.astype(BF16), v_ref[0, ks, vs])
            m_sc[...] = m_new

        def body(j, carry):
            step(j, False)
            return carry

        lax.fori_loop(0, qi, body, 0)
        step(qi, True)
        o_ref[0, :, vs] = (acc_sc[...] / l_sc[...]).astype(BF16)


def _flash_attention(q, k, v, *, t):
    b, l, _ = q.shape
    dv = N_HEADS * V_HEAD_DIM
    return pl.pallas_call(
        functools.partial(_flash_kernel, t=t),
        grid=(b, l // t),
        in_specs=[pl.BlockSpec((1, t, N_HEADS * HEAD_PAD), lambda i, j: (i, j, 0)),
                  pl.BlockSpec((1, l, N_HEADS * HEAD_PAD), lambda i, j: (i, 0, 0)),
                  pl.BlockSpec((1, l, dv), lambda i, j: (i, 0, 0))],
        out_specs=pl.BlockSpec((1, t, dv), lambda i, j: (i, j, 0)),
        out_shape=jax.ShapeDtypeStruct((b, l, dv), BF16),
        scratch_shapes=[pltpu.VMEM((t, 1), F32), pltpu.VMEM((t, 1), F32), pltpu.VMEM((t, V_HEAD_DIM), F32)],
        compiler_params=pltpu.CompilerParams(dimension_semantics=("arbitrary", "arbitrary"),
                                             vmem_limit_bytes=VMEM_LIMIT),
        name="flash_attention",
    )(q, k, v)


def _latent_attn_kernel(q_ref, ckv_p_ref, kr_p_ref, ckv_n_ref, kr_n_ref, wukt_ref, wuv_ref, o_ref, *, past):
    lq = q_ref.shape[1]
    ckv_p = ckv_p_ref[0].astype(BF16)
    kr_p = kr_p_ref[0].astype(BF16)
    ckv_n = ckv_n_ref[0].astype(BF16)
    kr_n = kr_n_ref[0].astype(BF16)
    q_lat, q_rope = [], []
    for h in range(N_HEADS):
        q_nope = q_ref[0, :, h * HEAD_PAD:h * HEAD_PAD + QK_NOPE_DIM]
        q_lat.append(_dot(q_nope, wukt_ref[h]).astype(BF16))
        q_rope.append(q_ref[0, :, h * HEAD_PAD + ROPE_OFF:h * HEAD_PAD + ROPE_OFF + QK_ROPE_DIM])
    q_lat = jnp.concatenate(q_lat, axis=0)
    q_rope = jnp.concatenate(q_rope, axis=0)
    rows = N_HEADS * lq
    s_p = (_dot_nt(q_lat, ckv_p) + _dot_nt(q_rope, kr_p)) * ATTN_SCALE
    s_n = (_dot_nt(q_lat, ckv_n) + _dot_nt(q_rope, kr_n)) * ATTN_SCALE

    def q_chunk(n):
        return jnp.concatenate([_chunk_of(past + lax.broadcasted_iota(jnp.int32, (lq, n), 0))] * N_HEADS, axis=0)

    k_chunk_p = _chunk_of(lax.broadcasted_iota(jnp.int32, (rows, past), 1))
    s_p = jnp.where(k_chunk_p <= q_chunk(past), s_p, MASK_VALUE)
    k_chunk_n = _chunk_of(past + lax.broadcasted_iota(jnp.int32, (rows, lq), 1))
    s_n = jnp.where(k_chunk_n <= q_chunk(lq), s_n, MASK_VALUE)
    m = jnp.maximum(jnp.max(s_p, axis=-1, keepdims=True), jnp.max(s_n, axis=-1, keepdims=True))
    p_p = jnp.exp(s_p - m)
    p_n = jnp.exp(s_n - m)
    denom = jnp.sum(p_p, axis=-1, keepdims=True) + jnp.sum(p_n, axis=-1, keepdims=True)
    p_p = (p_p / denom).astype(BF16)
    p_n = (p_n / denom).astype(BF16)
    o_lat = (_dot(p_p, ckv_p) + _dot(p_n, ckv_n)).astype(BF16)
    for h in range(N_HEADS):
        o_h = _dot(o_lat[h * lq:(h + 1) * lq], wuv_ref[h])
        o_ref[0, :, h * V_HEAD_DIM:(h + 1) * V_HEAD_DIM] = o_h.astype(BF16)


def _latent_attention(q, ckv_past, kr_past, ckv_new, kr_new, w_ukt, w_uv):
    b, lq, _ = q.shape
    past = ckv_past.shape[1]
    dv = N_HEADS * V_HEAD_DIM

    def seq(a):
        return pl.BlockSpec((1,) + a.shape[1:], lambda i: (i, 0, 0))

    def full(a):
        return pl.BlockSpec(a.shape, lambda i: (0,) * a.ndim)

    return pl.pallas_call(
        functools.partial(_latent_attn_kernel, past=past),
        grid=(b,),
        in_specs=[seq(q), seq(ckv_past), seq(kr_past), seq(ckv_new), seq(kr_new), full(w_ukt), full(w_uv)],
        out_specs=pl.BlockSpec((1, lq, dv), lambda i: (i, 0, 0)),
        out_shape=jax.ShapeDtypeStruct((b, lq, dv), BF16),
        compiler_params=pltpu.CompilerParams(dimension_semantics=("arbitrary",), vmem_limit_bytes=VMEM_LIMIT),
        name="latent_attention",
    )(q, ckv_past, kr_past, ckv_new, kr_new, w_ukt, w_uv)


def _mixer_out_kernel(x_ref, o_ref, br_ref, wg_ref, bg_ref, wmla_ref, wconf_ref, wsc_ref, wmix_ref, g_ref, b_ref,
                      y_ref, *, alpha):
    x = x_ref[...]
    d = x.shape[-1]
    gate = _sigmoid(_dot(x.astype(BF16), wg_ref[...]) + bg_ref[...])
    merged = gate[:, :d] * _dot(o_ref[...], wmla_ref[...])
    merged = merged + gate[:, d:2 * d] * _dot(br_ref[:, :CONF_CH], wconf_ref[...])
    merged = merged + gate[:, 2 * d:] * _dot(br_ref[:, CONF_CH:], wsc_ref[...])
    y = _dot(merged.astype(BF16), wmix_ref[...])
    y_ref[...] = _layer_norm(alpha * x + y, g_ref[...], b_ref[...])


def _mixer_out(x, o, br, wl, *, tm, alpha):
    m, d = x.shape
    weights = (wl["w_gate"], wl["b_gate"], wl["w_mla_out"], wl["w_conf_out"], wl["w_sc_out"], wl["w_mix"],
               wl["ln1_g"], wl["ln1_b"])

    def row(c):
        return pl.BlockSpec((tm, c), lambda i: (i, 0))

    def full(a):
        return pl.BlockSpec(a.shape, lambda i: (0,) * a.ndim)

    return pl.pallas_call(
        functools.partial(_mixer_out_kernel, alpha=alpha),
        grid=(m // tm,),
        in_specs=[row(d), row(o.shape[1]), row(br.shape[1])] + [full(w) for w in weights],
        out_specs=row(d),
        out_shape=jax.ShapeDtypeStruct((m, d), F32),
        compiler_params=pltpu.CompilerParams(dimension_semantics=("arbitrary",), vmem_limit_bytes=VMEM_LIMIT),
        name="mixer_out",
    )(x, o, br, *weights)


def _ffn_kernel(x_ref, w1_ref, b1_ref, w2_ref, b2_ref, g_ref, b_ref, y_ref, *, alpha):
    x = x_ref[...]
    h = jnp.maximum(_dot(x.astype(BF16), w1_ref[...]) + b1_ref[...], 0.0)
    y = _dot((h * h).astype(BF16), w2_ref[...]) + b2_ref[...]
    y_ref[...] = _layer_norm(alpha * x + y, g_ref[...], b_ref[...])


def _ffn(x, wl, *, tm, alpha):
    m, d = x.shape
    weights = (wl["w_ff1"], wl["b_ff1"], wl["w_ff2"], wl["b_ff2"], wl["ln2_g"], wl["ln2_b"])

    def row(c):
        return pl.BlockSpec((tm, c), lambda i: (i, 0))

    def full(a):
        return pl.BlockSpec(a.shape, lambda i: (0,) * a.ndim, pipeline_mode=pl.Buffered(1))

    return pl.pallas_call(
        functools.partial(_ffn_kernel, alpha=alpha),
        grid=(m // tm,),
        in_specs=[row(d)] + [full(w) for w in weights],
        out_specs=row(d),
        out_shape=jax.ShapeDtypeStruct((m, d), F32),
        compiler_params=pltpu.CompilerParams(dimension_semantics=("arbitrary",), vmem_limit_bytes=VMEM_LIMIT),
        name="ffn",
    )(x, *weights)


def _rope_tables(pos):
    half = QK_ROPE_DIM // 2
    inv = ROPE_THETA ** (-jnp.arange(half, dtype=F32) / half)
    ang = pos.astype(F32)[:, None] * inv[None, :]
    cos, sin = jnp.cos(ang), jnp.sin(ang)
    n = pos.shape[0]
    zeros = lambda c: jnp.zeros((n, c), F32)
    pad = HEAD_PAD - ROPE_OFF - QK_ROPE_DIM
    cc = jnp.concatenate([jnp.ones((n, ROPE_OFF), F32), cos, cos, zeros(pad)], axis=1)
    sa = jnp.concatenate([zeros(ROPE_OFF), -sin, zeros(half), zeros(pad)], axis=1)
    sb = jnp.concatenate([zeros(ROPE_OFF), zeros(half), sin, zeros(pad)], axis=1)
    return cc, sa, sb


def _layer_weights(i, p):
    d = p["w_in"].shape[1]
    w_in = p["w_in"][i]
    offs = [0, Q_LORA, Q_LORA + KV_LORA, Q_LORA + KV_LORA + QK_ROPE_DIM]
    offs.append(offs[-1] + 2 * CONF_CH)
    offs.append(offs[-1] + 3 * SC_CH)
    w_kr = jnp.pad(w_in[:, offs[2]:offs[3]], ((0, 0), (ROPE_OFF, LANES - ROPE_OFF - QK_ROPE_DIM)))
    w_a = jnp.concatenate([w_in[:, :offs[2]], w_kr, w_in[:, offs[3]:offs[5]]], axis=1)
    head_pad = ((0, 0), (0, 0), (0, HEAD_PAD - QK_NOPE_DIM - QK_ROPE_DIM))
    w_q = jnp.pad(p["w_uq"][i], head_pad).reshape(Q_LORA, N_HEADS * HEAD_PAD)
    w_k = jnp.pad(p["w_uk"][i], ((0, 0), (0, 0), (0, HEAD_PAD - QK_NOPE_DIM))).reshape(KV_LORA, N_HEADS * HEAD_PAD)
    w_v = p["w_uv"][i].reshape(KV_LORA, N_HEADS * V_HEAD_DIM)
    row = lambda a: a.reshape(1, -1).astype(F32)
    return {
        "w_a": w_a.astype(BF16), "q_g": row(p["q_norm_g"][i]), "w_q": w_q.astype(BF16),
        "kv_g": row(p["kv_norm_g"][i]), "w_kv": jnp.concatenate([w_k, w_v], axis=1).astype(BF16),
        "conf_w": p["conf_dw_w"][i], "conf_b": row(p["conf_dw_b"][i]), "conf_lg": row(p["conf_ln_g"][i]),
        "conf_lb": row(p["conf_ln_b"][i]), "sc_w": p["sc_dw_w"][i],
        "w_ukt": jnp.transpose(p["w_uk"][i], (1, 2, 0)).astype(BF16),
        "w_uv_h": jnp.transpose(p["w_uv"][i], (1, 0, 2)).astype(BF16),
        "w_gate": w_in[:, offs[5]:].astype(BF16), "b_gate": row(p["b_gate"][i]),
        "w_mla_out": p["w_mla_out"][i].astype(BF16), "w_conf_out": p["w_conf_out"][i].astype(BF16),
        "w_sc_out": p["w_sc_out"][i].astype(BF16), "w_mix": p["w_mix_out"][i].astype(BF16),
        "ln1_g": row(p["ln1_g"][i]), "ln1_b": row(p["ln1_b"][i]),
        "w_ff1": p["w_ff1"][i].astype(BF16), "b_ff1": row(p["b_ff1"][i]),
        "w_ff2": p["w_ff2"][i].astype(BF16), "b_ff2": row(p["b_ff2"][i]),
        "ln2_g": row(p["ln2_g"][i]), "ln2_b": row(p["ln2_b"][i]),
    }


def _tiles(b, l):
    tl = min(l, 512)
    bt = 1 if tl >= 256 else b
    tm = min(b * l, 512)
    return bt, tl, tm


def _run(x, past_ckv, past_kr, past_conf, past_sc, layers, alpha):
    b, l, d = x.shape
    depth = len(layers)
    past = 0 if past_ckv is None else past_ckv.shape[2]
    bt, tl, tm = _tiles(b, l)
    tabs = _rope_tables(past + jnp.arange(l))
    ckvs, krs, confs, scs = [], [], [], []
    for i in range(depth):
        wl = layers[i]
        conf0 = jnp.zeros((b, CONF_WIDTH - 1, CONF_CH), F32) if past_conf is None else past_conf[i]
        sc0 = jnp.zeros((b, SC_WIDTH - 1, SC_CH), F32) if past_sc is None else past_sc[i]
        q, k, v, c_kv, k_rope, br, n_conf, n_sc = _mixer_in(x, tabs, conf0, sc0, wl, bt=bt, tl=tl)
        if past_ckv is None:
            o = _flash_attention(q, k, v, t=tl)
        else:
            o = _latent_attention(q, past_ckv[i], past_kr[i], c_kv, k_rope, wl["w_ukt"], wl["w_uv_h"])
        x1 = _mixer_out(x.reshape(b * l, d), o.reshape(b * l, -1), br.reshape(b * l, -1), wl, tm=tm, alpha=alpha)
        x = _ffn(x1, wl, tm=tm, alpha=alpha).reshape(b, l, d)
        ckvs.append(c_kv)
        krs.append(k_rope)
        confs.append(n_conf)
        scs.append(n_sc)
    return x, jnp.stack(ckvs), jnp.stack(krs), jnp.stack(confs), jnp.stack(scs)


def kernel(x_prompt, x_sample, cache_ckv, cache_krope, state_conf, state_sc, w_in, b_gate, q_norm_g, w_uq, kv_norm_g, w_uk, w_uv, w_mla_out, conf_dw_w, conf_dw_b, conf_ln_g, conf_ln_b, w_conf_out, sc_dw_w, w_sc_out, w_mix_out, ln1_g, ln1_b, w_ff1, b_ff1, w_ff2, b_ff2, ln2_g, ln2_b):
    params = dict(w_in=w_in, b_gate=b_gate, q_norm_g=q_norm_g, w_uq=w_uq, kv_norm_g=kv_norm_g, w_uk=w_uk, w_uv=w_uv,
                  w_mla_out=w_mla_out, conf_dw_w=conf_dw_w, conf_dw_b=conf_dw_b, conf_ln_g=conf_ln_g,
                  conf_ln_b=conf_ln_b, w_conf_out=w_conf_out, sc_dw_w=sc_dw_w, w_sc_out=w_sc_out,
                  w_mix_out=w_mix_out, ln1_g=ln1_g, ln1_b=ln1_b, w_ff1=w_ff1, b_ff1=b_ff1, w_ff2=w_ff2, b_ff2=b_ff2,
                  ln2_g=ln2_g, ln2_b=ln2_b)
    depth = w_in.shape[0]
    alpha = (2 * depth) ** 0.25
    layers = [_layer_weights(i, params) for i in range(depth)]
    y_p, ckv_p, kr_p, conf_p, sc_p = _run(x_prompt, None, None, None, None, layers, alpha)
    y_s, ckv_s, kr_s, conf_s, sc_s = _run(x_sample, cache_ckv, cache_krope, state_conf, state_sc, layers, alpha)
    return (y_p, y_s, ckv_p, kr_p, conf_p, sc_p, ckv_s, kr_s, conf_s, sc_s)
```

```python
import functools

import jax
import jax.numpy as jnp
from jax import lax
from jax.experimental import pallas as pl
from jax.experimental.pallas import tpu as pltpu

CHUNK = 64
CHUNK_SHIFT = 6
assert 1 << CHUNK_SHIFT == CHUNK
N_HEADS = 8
QK_NOPE_DIM = 64
QK_ROPE_DIM = 32
V_HEAD_DIM = 64
Q_LORA = 384
KV_LORA = 256
ROPE_THETA = 10000.0
ATTN_SCALE = (QK_NOPE_DIM + QK_ROPE_DIM) ** -0.5
CONF_CH = 256
CONF_WIDTH = 31
SC_CH = 256
SC_WIDTH = 3
N_BRANCH = 3
LN_EPS = 1e-5
RMS_EPS = 1e-6
MASK_VALUE = -1e30
LOG2_E = 1.4426950408889634

SUBLANES = 8
LANES = 128
HEAD_PAD = LANES
ROPE_OFF = QK_NOPE_DIM
CONF_HALO = 32
SC_HALO = 8
VMEM_LIMIT = 56 * 1024 * 1024

BF16 = jnp.bfloat16
F32 = jnp.float32


def _dot(a, b):
    return jnp.dot(a, b, preferred_element_type=F32)


def _dot_nt(a, b):
    return lax.dot_general(a, b, (((1,), (1,)), ((), ())), preferred_element_type=F32)


def _sigmoid(x):
    return 1.0 / (1.0 + jnp.exp(-x))


def _layer_norm(x, g, b):
    mu = jnp.mean(x, axis=-1, keepdims=True)
    xc = x - mu
    var = jnp.mean(xc * xc, axis=-1, keepdims=True)
    return xc * lax.rsqrt(var + LN_EPS) * g + b


def _rms_norm(x, g):
    ms = jnp.mean(x * x, axis=-1, keepdims=True)
    return x * lax.rsqrt(ms + RMS_EPS) * g


def _rope_slab(x, cc, sa, sb):
    up = pltpu.roll(x, LANES - QK_ROPE_DIM // 2, 1)
    down = pltpu.roll(x, QK_ROPE_DIM // 2, 1)
    return x * cc + up * sa + down * sb


def _causal_dw_conv(buf, w_ref, base, width, tl):
    out = None
    for res in range(SUBLANES):
        taps = [k for k in range(width) if (base + k) % SUBLANES == res]
        if not taps:
            continue
        k0 = taps[0]
        shifted = buf[:, base + k0:base + taps[-1] + tl, :]
        group = None
        for k in taps:
            term = shifted[:, k - k0:k - k0 + tl, :] * w_ref[k:k + 1, :]
            group = term if group is None else group + term
        out = group if out is None else out + group
    return out


def _mixer_in_kernel(*refs, bt, tl, prompt):
    (x_ref, cc_ref, sa_ref, sb_ref, conf0_ref, sc0_ref, wa_ref, qg_ref, wq_ref, kvg_ref, cw_ref, cb_ref, clg_ref,
     clb_ref, sw_ref) = refs[:15]
    if prompt:
        wkv_ref, q_ref, k_ref, v_ref, ckv_ref, kr_ref, br_ref, nconf_ref, nsc_ref, ubuf, zbuf = refs[15:]
    else:
        q_ref, ckv_ref, kr_ref, br_ref, nconf_ref, nsc_ref, ubuf, zbuf = refs[15:]
    lt = pl.program_id(1)
    rows = bt * tl
    d_model = x_ref.shape[-1]

    @pl.when(lt == 0)
    def _():
        ubuf[:, CONF_HALO - (CONF_WIDTH - 1):CONF_HALO, :] = conf0_ref[...]
        zbuf[:, SC_HALO - (SC_WIDTH - 1):SC_HALO, :] = sc0_ref[...]

    x = x_ref[...].reshape(rows, d_model)
    proj = _dot(x.astype(BF16), wa_ref[...])
    o_kv = Q_LORA
    o_kr = o_kv + KV_LORA
    o_conf = o_kr + LANES
    o_sc = o_conf + 2 * CONF_CH
    cc = jnp.concatenate([cc_ref[...]] * bt, axis=0) if bt > 1 else cc_ref[...]
    sa = jnp.concatenate([sa_ref[...]] * bt, axis=0) if bt > 1 else sa_ref[...]
    sb = jnp.concatenate([sb_ref[...]] * bt, axis=0) if bt > 1 else sb_ref[...]

    qn = _rms_norm(proj[:, :Q_LORA], qg_ref[...])
    q_pre = _dot(qn.astype(BF16), wq_ref[...])
    for h in range(N_HEADS):
        sl = slice(h * HEAD_PAD, (h + 1) * HEAD_PAD)
        q_h = _rope_slab(q_pre[:, sl], cc, sa, sb)
        if prompt:
            q_ref[0, sl, :] = q_h.T.astype(BF16)
        else:
            q_ref[:, :, sl] = q_h.astype(BF16).reshape(bt, tl, HEAD_PAD)
    c_kv = _rms_norm(proj[:, o_kv:o_kr], kvg_ref[...])
    ckv_ref[...] = c_kv.reshape(bt, tl, KV_LORA)
    kr = _rope_slab(proj[:, o_kr:o_conf], cc, sa, sb)
    kr_ref[...] = kr[:, ROPE_OFF:ROPE_OFF + QK_ROPE_DIM].reshape(bt, tl, QK_ROPE_DIM)
    if prompt:
        kv_up = _dot(c_kv.astype(BF16), wkv_ref[...])
        for h in range(N_HEADS):
            sl = slice(h * HEAD_PAD, (h + 1) * HEAD_PAD)
            k_ref[0, :, sl] = (kv_up[:, sl] + kr).astype(BF16)
        v_ref[0] = kv_up[:, N_HEADS * HEAD_PAD:].T.astype(BF16)

    u = proj[:, o_conf:o_conf + CONF_CH] * _sigmoid(proj[:, o_conf + CONF_CH:o_sc])
    ubuf[:, CONF_HALO:CONF_HALO + tl, :] = u.reshape(bt, tl, CONF_CH)
    cv = _causal_dw_conv(ubuf, cw_ref, CONF_HALO - (CONF_WIDTH - 1), CONF_WIDTH, tl)
    cv = cv.reshape(rows, CONF_CH) + cb_ref[...]
    cn = _layer_norm(cv, clg_ref[...], clb_ref[...])
    br_ref[:, :, :CONF_CH] = (cn * _sigmoid(cn)).astype(BF16).reshape(bt, tl, CONF_CH)
    tail = ubuf[:, tl:tl + CONF_HALO, :]
    ubuf[:, :CONF_HALO, :] = tail
    nconf_ref[...] = tail[:, CONF_HALO - (CONF_WIDTH - 1):, :]

    gb = proj[:, o_sc:o_sc + SC_CH]
    z = proj[:, o_sc + SC_CH:o_sc + 2 * SC_CH] * proj[:, o_sc + 2 * SC_CH:o_sc + 3 * SC_CH]
    zbuf[:, SC_HALO:SC_HALO + tl, :] = z.reshape(bt, tl, SC_CH)
    sv = _causal_dw_conv(zbuf, sw_ref, SC_HALO - (SC_WIDTH - 1), SC_WIDTH, tl)
    br_ref[:, :, CONF_CH:] = (gb * sv.reshape(rows, SC_CH)).astype(BF16).reshape(bt, tl, SC_CH)
    tail = zbuf[:, tl:tl + SC_HALO, :]
    zbuf[:, :SC_HALO, :] = tail
    nsc_ref[...] = tail[:, SC_HALO - (SC_WIDTH - 1):, :]


def _mixer_in(x, rope_tabs, conf0, sc0, wl, *, bt, tl, prompt):
    b, l, d = x.shape
    assert bt == 1 or not prompt
    grid = (b // bt, l // tl)

    def full(a):
        return pl.BlockSpec(a.shape, lambda i, j: (0,) * a.ndim)

    def seq(c):
        return pl.BlockSpec((bt, tl, c), lambda i, j: (i, j, 0))

    def seq_t(c):
        return pl.BlockSpec((1, c, tl), lambda i, j: (i, 0, j))

    def state(r, c):
        return pl.BlockSpec((bt, r, c), lambda i, j: (i, 0, 0))

    tab = pl.BlockSpec((tl, LANES), lambda i, j: (j, 0))
    weights = [wl["w_a"], wl["q_g"], wl["w_q"], wl["kv_g"], wl["conf_w"], wl["conf_b"], wl["conf_lg"], wl["conf_lb"],
               wl["sc_w"]]
    dq, dv = N_HEADS * HEAD_PAD, N_HEADS * V_HEAD_DIM
    common_shape = [
        jax.ShapeDtypeStruct((b, l, KV_LORA), F32),
        jax.ShapeDtypeStruct((b, l, QK_ROPE_DIM), F32),
        jax.ShapeDtypeStruct((b, l, CONF_CH + SC_CH), BF16),
        jax.ShapeDtypeStruct((b, CONF_WIDTH - 1, CONF_CH), F32),
        jax.ShapeDtypeStruct((b, SC_WIDTH - 1, SC_CH), F32),
    ]
    common_specs = [seq(KV_LORA), seq(QK_ROPE_DIM), seq(CONF_CH + SC_CH), state(CONF_WIDTH - 1, CONF_CH),
                    state(SC_WIDTH - 1, SC_CH)]
    if prompt:
        weights.append(wl["w_kv"])
        out_shape = [jax.ShapeDtypeStruct((b, dq, l), BF16), jax.ShapeDtypeStruct((b, l, dq), BF16),
                     jax.ShapeDtypeStruct((b, dv, l), BF16)] + common_shape
        out_specs = [seq_t(dq), seq(dq), seq_t(dv)] + common_specs
    else:
        out_shape = [jax.ShapeDtypeStruct((b, l, dq), BF16)] + common_shape
        out_specs = [seq(dq)] + common_specs
    return pl.pallas_call(
        functools.partial(_mixer_in_kernel, bt=bt, tl=tl, prompt=prompt),
        grid=grid,
        in_specs=[seq(d), tab, tab, tab, state(CONF_WIDTH - 1, CONF_CH), state(SC_WIDTH - 1, SC_CH)]
        + [full(w) for w in weights],
        out_specs=out_specs,
        out_shape=out_shape,
        scratch_shapes=[pltpu.VMEM((bt, CONF_HALO + tl, CONF_CH), F32), pltpu.VMEM((bt, SC_HALO + tl, SC_CH), F32)],
        compiler_params=pltpu.CompilerParams(dimension_semantics=("arbitrary", "arbitrary"),
                                             vmem_limit_bytes=VMEM_LIMIT),
        name="mixer_in",
    )(x, *rope_tabs, conf0, sc0, *weights)


def _chunk_of(pos):
    return pos >> CHUNK_SHIFT


def _chunk_mask_t(tk, tq):
    kc = _chunk_of(lax.broadcasted_iota(jnp.int32, (tk, tq), 0))
    qc = _chunk_of(lax.broadcasted_iota(jnp.int32, (tk, tq), 1))
    return kc <= qc


def _flash_kernel(qt_ref, k_ref, vt_ref, o_ref, m_sc, l_sc, acc_sc, *, t):
    qi = pl.program_id(1)
    exp2_scale = ATTN_SCALE * LOG2_E
    m_sc[...] = jnp.full(m_sc.shape, MASK_VALUE, F32)
    l_sc[...] = jnp.zeros(l_sc.shape, F32)
    acc_sc[...] = jnp.zeros(acc_sc.shape, F32)

    def step(j, masked):
        ks = pl.ds(pl.multiple_of(j * t, t), t)
        for h in range(N_HEADS):
            hs = slice(h * HEAD_PAD, (h + 1) * HEAD_PAD)
            vs = slice(h * V_HEAD_DIM, (h + 1) * V_HEAD_DIM)
            s = _dot(k_ref[0, ks, hs], qt_ref[0, hs, :])
            if masked:
                s = jnp.where(_chunk_mask_t(t, t), s, MASK_VALUE)
            m_old = m_sc[h]
            m_new = jnp.maximum(m_old, jnp.max(s, axis=0, keepdims=True))
            alpha = jnp.exp2((m_old - m_new) * exp2_scale)
            p = jnp.exp2((s - m_new) * exp2_scale)
            l_sc[h] = alpha * l_sc[h] + jnp.sum(p, axis=0, keepdims=True)
            acc_sc[h] = alpha * acc_sc[h] + _dot(vt_ref[0, vs, ks], p.astype(BF16))
            m_sc[h] = m_new

    def body(j, carry):
        step(j, False)
        return carry

    lax.fori_loop(0, qi, body, 0)
    step(qi, True)
    for h in range(N_HEADS):
        vs = slice(h * V_HEAD_DIM, (h + 1) * V_HEAD_DIM)
        o_ref[0, :, vs] = (acc_sc[h] / l_sc[h]).T.astype(BF16)


def _flash_attention(qt, k, vt, *, t):
    b, l, dq = k.shape
    dv = vt.shape[1]
    return pl.pallas_call(
        functools.partial(_flash_kernel, t=t),
        grid=(b, l // t),
        in_specs=[pl.BlockSpec((1, dq, t), lambda i, j: (i, 0, j)),
                  pl.BlockSpec((1, l, dq), lambda i, j: (i, 0, 0)),
                  pl.BlockSpec((1, dv, l), lambda i, j: (i, 0, 0))],
        out_specs=pl.BlockSpec((1, t, dv), lambda i, j: (i, j, 0)),
        out_shape=jax.ShapeDtypeStruct((b, l, dv), BF16),
        scratch_shapes=[pltpu.VMEM((N_HEADS, 1, t), F32), pltpu.VMEM((N_HEADS, 1, t), F32),
                        pltpu.VMEM((N_HEADS, V_HEAD_DIM, t), F32)],
        compiler_params=pltpu.CompilerParams(dimension_semantics=("arbitrary", "arbitrary"),
                                             vmem_limit_bytes=VMEM_LIMIT),
        name="flash_attention",
    )(qt, k, vt)


def _latent_attn_kernel(q_ref, ckv_p_ref, kr_p_ref, ckv_n_ref, kr_n_ref, wukt_ref, wuv_ref, o_ref, *, past):
    lq = q_ref.shape[1]
    ckv_p = ckv_p_ref[0].astype(BF16)
    kr_p = kr_p_ref[0].astype(BF16)
    ckv_n = ckv_n_ref[0].astype(BF16)
    kr_n = kr_n_ref[0].astype(BF16)
    q_lat, q_rope = [], []
    for h in range(N_HEADS):
        q_nope = q_ref[0, :, h * HEAD_PAD:h * HEAD_PAD + QK_NOPE_DIM]
        q_lat.append(_dot(q_nope, wukt_ref[h]).astype(BF16))
        q_rope.append(q_ref[0, :, h * HEAD_PAD + ROPE_OFF:h * HEAD_PAD + ROPE_OFF + QK_ROPE_DIM])
    q_lat = jnp.concatenate(q_lat, axis=0)
    q_rope = jnp.concatenate(q_rope, axis=0)
    rows = N_HEADS * lq
    s_p = (_dot_nt(q_lat, ckv_p) + _dot_nt(q_rope, kr_p)) * ATTN_SCALE
    s_n = (_dot_nt(q_lat, ckv_n) + _dot_nt(q_rope, kr_n)) * ATTN_SCALE

    def q_chunk(n):
        return jnp.concatenate([_chunk_of(past + lax.broadcasted_iota(jnp.int32, (lq, n), 0))] * N_HEADS, axis=0)

    k_chunk_p = _chunk_of(lax.broadcasted_iota(jnp.int32, (rows, past), 1))
    s_p = jnp.where(k_chunk_p <= q_chunk(past), s_p, MASK_VALUE)
    k_chunk_n = _chunk_of(past + lax.broadcasted_iota(jnp.int32, (rows, lq), 1))
    s_n = jnp.where(k_chunk_n <= q_chunk(lq), s_n, MASK_VALUE)
    m = jnp.maximum(jnp.max(s_p, axis=-1, keepdims=True), jnp.max(s_n, axis=-1, keepdims=True))
    p_p = jnp.exp(s_p - m)
    p_n = jnp.exp(s_n - m)
    denom = jnp.sum(p_p, axis=-1, keepdims=True) + jnp.sum(p_n, axis=-1, keepdims=True)
    p_p = (p_p / denom).astype(BF16)
    p_n = (p_n / denom).astype(BF16)
    o_lat = (_dot(p_p, ckv_p) + _dot(p_n, ckv_n)).astype(BF16)
    for h in range(N_HEADS):
        o_h = _dot(o_lat[h * lq:(h + 1) * lq], wuv_ref[h])
        o_ref[0, :, h * V_HEAD_DIM:(h + 1) * V_HEAD_DIM] = o_h.astype(BF16)


def _latent_attention(q, ckv_past, kr_past, ckv_new, kr_new, w_ukt, w_uv):
    b, lq, _ = q.shape
    past = ckv_past.shape[1]
    dv = N_HEADS * V_HEAD_DIM

    def seq(a):
        return pl.BlockSpec((1,) + a.shape[1:], lambda i: (i, 0, 0))

    def full(a):
        return pl.BlockSpec(a.shape, lambda i: (0,) * a.ndim)

    return pl.pallas_call(
        functools.partial(_latent_attn_kernel, past=past),
        grid=(b,),
        in_specs=[seq(q), seq(ckv_past), seq(kr_past), seq(ckv_new), seq(kr_new), full(w_ukt), full(w_uv)],
        out_specs=pl.BlockSpec((1, lq, dv), lambda i: (i, 0, 0)),
        out_shape=jax.ShapeDtypeStruct((b, lq, dv), BF16),
        compiler_params=pltpu.CompilerParams(dimension_semantics=("arbitrary",), vmem_limit_bytes=VMEM_LIMIT),
        name="latent_attention",
    )(q, ckv_past, kr_past, ckv_new, kr_new, w_ukt, w_uv)


def _mixer_out_kernel(x_ref, o_ref, br_ref, wg_ref, bg_ref, wmla_ref, wconf_ref, wsc_ref, wmix_ref, g_ref, b_ref,
                      y_ref, *, alpha):
    x = x_ref[...]
    d = x.shape[-1]
    gate = _sigmoid(_dot(x.astype(BF16), wg_ref[...]) + bg_ref[...])
    merged = gate[:, :d] * _dot(o_ref[...], wmla_ref[...])
    merged = merged + gate[:, d:2 * d] * _dot(br_ref[:, :CONF_CH], wconf_ref[...])
    merged = merged + gate[:, 2 * d:] * _dot(br_ref[:, CONF_CH:], wsc_ref[...])
    y = _dot(merged.astype(BF16), wmix_ref[...])
    y_ref[...] = _layer_norm(alpha * x + y, g_ref[...], b_ref[...])


def _mixer_out(x, o, br, wl, *, tm, alpha):
    m, d = x.shape
    weights = (wl["w_gate"], wl["b_gate"], wl["w_mla_out"], wl["w_conf_out"], wl["w_sc_out"], wl["w_mix"],
               wl["ln1_g"], wl["ln1_b"])

    def row(c):
        return pl.BlockSpec((tm, c), lambda i: (i, 0))

    def full(a):
        return pl.BlockSpec(a.shape, lambda i: (0,) * a.ndim)

    return pl.pallas_call(
        functools.partial(_mixer_out_kernel, alpha=alpha),
        grid=(m // tm,),
        in_specs=[row(d), row(o.shape[1]), row(br.shape[1])] + [full(w) for w in weights],
        out_specs=row(d),
        out_shape=jax.ShapeDtypeStruct((m, d), F32),
        compiler_params=pltpu.CompilerParams(dimension_semantics=("arbitrary",), vmem_limit_bytes=VMEM_LIMIT),
        name="mixer_out",
    )(x, o, br, *weights)


def _ffn_kernel(x_ref, w1_ref, b1_ref, w2_ref, b2_ref, g_ref, b_ref, y_ref, *, alpha):
    x = x_ref[...]
    h = jnp.maximum(_dot(x.astype(BF16), w1_ref[...]) + b1_ref[...], 0.0)
    y = _dot((h * h).astype(BF16), w2_ref[...]) + b2_ref[...]
    y_ref[...] = _layer_norm(alpha * x + y, g_ref[...], b_ref[...])


def _ffn(x, wl, *, tm, alpha):
    m, d = x.shape
    weights = (wl["w_ff1"], wl["b_ff1"], wl["w_ff2"], wl["b_ff2"], wl["ln2_g"], wl["ln2_b"])

    def row(c):
        return pl.BlockSpec((tm, c), lambda i: (i, 0))

    def full(a):
        return pl.BlockSpec(a.shape, lambda i: (0,) * a.ndim, pipeline_mode=pl.Buffered(1))

    return pl.pallas_call(
        functools.partial(_ffn_kernel, alpha=alpha),
        grid=(m // tm,),
        in_specs=[row(d)] + [full(w) for w in weights],
        out_specs=row(d),
        out_shape=jax.ShapeDtypeStruct((m, d), F32),
        compiler_params=pltpu.CompilerParams(dimension_semantics=("arbitrary",), vmem_limit_bytes=VMEM_LIMIT),
        name="ffn",
    )(x, *weights)


def _rope_tables(pos):
    half = QK_ROPE_DIM // 2
    inv = ROPE_THETA ** (-jnp.arange(half, dtype=F32) / half)
    ang = pos.astype(F32)[:, None] * inv[None, :]
    cos, sin = jnp.cos(ang), jnp.sin(ang)
    n = pos.shape[0]
    zeros = lambda c: jnp.zeros((n, c), F32)
    pad = HEAD_PAD - ROPE_OFF - QK_ROPE_DIM
    cc = jnp.concatenate([jnp.ones((n, ROPE_OFF), F32), cos, cos, zeros(pad)], axis=1)
    sa = jnp.concatenate([zeros(ROPE_OFF), -sin, zeros(half), zeros(pad)], axis=1)
    sb = jnp.concatenate([zeros(ROPE_OFF), zeros(half), sin, zeros(pad)], axis=1)
    return cc, sa, sb


def _layer_weights(i, p):
    d = p["w_in"].shape[1]
    w_in = p["w_in"][i]
    offs = [0, Q_LORA, Q_LORA + KV_LORA, Q_LORA + KV_LORA + QK_ROPE_DIM]
    offs.append(offs[-1] + 2 * CONF_CH)
    offs.append(offs[-1] + 3 * SC_CH)
    w_kr = jnp.pad(w_in[:, offs[2]:offs[3]], ((0, 0), (ROPE_OFF, LANES - ROPE_OFF - QK_ROPE_DIM)))
    w_a = jnp.concatenate([w_in[:, :offs[2]], w_kr, w_in[:, offs[3]:offs[5]]], axis=1)
    head_pad = ((0, 0), (0, 0), (0, HEAD_PAD - QK_NOPE_DIM - QK_ROPE_DIM))
    w_q = jnp.pad(p["w_uq"][i], head_pad).reshape(Q_LORA, N_HEADS * HEAD_PAD)
    w_k = jnp.pad(p["w_uk"][i], ((0, 0), (0, 0), (0, HEAD_PAD - QK_NOPE_DIM))).reshape(KV_LORA, N_HEADS * HEAD_PAD)
    w_v = p["w_uv"][i].reshape(KV_LORA, N_HEADS * V_HEAD_DIM)
    row = lambda a: a.reshape(1, -1).astype(F32)
    return {
        "w_a": w_a.astype(BF16), "q_g": row(p["q_norm_g"][i]), "w_q": w_q.astype(BF16),
        "kv_g": row(p["kv_norm_g"][i]), "w_kv": jnp.concatenate([w_k, w_v], axis=1).astype(BF16),
        "conf_w": p["conf_dw_w"][i], "conf_b": row(p["conf_dw_b"][i]), "conf_lg": row(p["conf_ln_g"][i]),
        "conf_lb": row(p["conf_ln_b"][i]), "sc_w": p["sc_dw_w"][i],
        "w_ukt": jnp.transpose(p["w_uk"][i], (1, 2, 0)).astype(BF16),
        "w_uv_h": jnp.transpose(p["w_uv"][i], (1, 0, 2)).astype(BF16),
        "w_gate": w_in[:, offs[5]:].astype(BF16), "b_gate": row(p["b_gate"][i]),
        "w_mla_out": p["w_mla_out"][i].astype(BF16), "w_conf_out": p["w_conf_out"][i].astype(BF16),
        "w_sc_out": p["w_sc_out"][i].astype(BF16), "w_mix": p["w_mix_out"][i].astype(BF16),
        "ln1_g": row(p["ln1_g"][i]), "ln1_b": row(p["ln1_b"][i]),
        "w_ff1": p["w_ff1"][i].astype(BF16), "b_ff1": row(p["b_ff1"][i]),
        "w_ff2": p["w_ff2"][i].astype(BF16), "b_ff2": row(p["b_ff2"][i]),
        "ln2_g": row(p["ln2_g"][i]), "ln2_b": row(p["ln2_b"][i]),
    }


def _tiles(b, l):
    tl = min(l, 512)
    bt = 1 if tl >= 256 else b
    tm = min(b * l, 512)
    return bt, tl, tm


def _run(x, past_ckv, past_kr, past_conf, past_sc, layers, alpha):
    b, l, d = x.shape
    depth = len(layers)
    past = 0 if past_ckv is None else past_ckv.shape[2]
    bt, tl, tm = _tiles(b, l)
    tabs = _rope_tables(past + jnp.arange(l))
    ckvs, krs, confs, scs = [], [], [], []
    for i in range(depth):
        wl = layers[i]
        conf0 = jnp.zeros((b, CONF_WIDTH - 1, CONF_CH), F32) if past_conf is None else past_conf[i]
        sc0 = jnp.zeros((b, SC_WIDTH - 1, SC_CH), F32) if past_sc is None else past_sc[i]
        if past_ckv is None:
            qt, k, vt, c_kv, k_rope, br, n_conf, n_sc = _mixer_in(x, tabs, conf0, sc0, wl, bt=bt, tl=tl, prompt=True)
            o = _flash_attention(qt, k, vt, t=tl)
        else:
            q, c_kv, k_rope, br, n_conf, n_sc = _mixer_in(x, tabs, conf0, sc0, wl, bt=bt, tl=tl, prompt=False)
            o = _latent_attention(q, past_ckv[i], past_kr[i], c_kv, k_rope, wl["w_ukt"], wl["w_uv_h"])
        x1 = _mixer_out(x.reshape(b * l, d), o.reshape(b * l, -1), br.reshape(b * l, -1), wl, tm=tm, alpha=alpha)
        x = _ffn(x1, wl, tm=tm, alpha=alpha).reshape(b, l, d)
        ckvs.append(c_kv)
        krs.append(k_rope)
        confs.append(n_conf)
        scs.append(n_sc)
    return x, jnp.stack(ckvs), jnp.stack(krs), jnp.stack(confs), jnp.stack(scs)


def kernel(x_prompt, x_sample, cache_ckv, cache_krope, state_conf, state_sc, w_in, b_gate, q_norm_g, w_uq, kv_norm_g, w_uk, w_uv, w_mla_out, conf_dw_w, conf_dw_b, conf_ln_g, conf_ln_b, w_conf_out, sc_dw_w, w_sc_out, w_mix_out, ln1_g, ln1_b, w_ff1, b_ff1, w_ff2, b_ff2, ln2_g, ln2_b):
    params = dict(w_in=w_in, b_gate=b_gate, q_norm_g=q_norm_g, w_uq=w_uq, kv_norm_g=kv_norm_g, w_uk=w_uk, w_uv=w_uv,
                  w_mla_out=w_mla_out, conf_dw_w=conf_dw_w, conf_dw_b=conf_dw_b, conf_ln_g=conf_ln_g,
                  conf_ln_b=conf_ln_b, w_conf_out=w_conf_out, sc_dw_w=sc_dw_w, w_sc_out=w_sc_out,
                  w_mix_out=w_mix_out, ln1_g=ln1_g, ln1_b=ln1_b, w_ff1=w_ff1, b_ff1=b_ff1, w_ff2=w_ff2, b_ff2=b_ff2,
                  ln2_g=ln2_g, ln2_b=ln2_b)
    depth = w_in.shape[0]
    alpha = (2 * depth) ** 0.25
    layers = [_layer_weights(i, params) for i in range(depth)]
    y_p, ckv_p, kr_p, conf_p, sc_p = _run(x_prompt, None, None, None, None, layers, alpha)
    y_s, ckv_s, kr_s, conf_s, sc_s = _run(x_sample, cache_ckv, cache_krope, state_conf, state_sc, layers, alpha)
    return (y_p, y_s, ckv_p, kr_p, conf_p, sc_p, ckv_s, kr_s, conf_s, sc_s)
```
